```python
import math
import numpy as np
import jax, jax.numpy as jnp
from jax import lax

D_MODEL = 1024
BATCH = 2
SEQ = 16384
DEPTH = 1
DEC_BATCH = 128
DEC_SEQ = 1
PAST_LEN = 8192
PAGE_SIZE = 128

HA = 8
DK = 64
QK = 2 * DK
DV = 2 * DK
D_ATTN = HA * DV
Q_BLOCK = 128
HM = 4
DH = 256
D_MLSTM = HM * DH
CONV_W = 4
MLSTM_CHUNK = 128
N_BUCKETS = 32
MAX_DISTANCE = 128
N_EXPERTS = 64
TOP_K = 8
N_GROUPS = 8
TOPK_GROUPS = 4
D_EXPERT = 256
D_SHARED = 256
ROUTED_SCALE = 2.5
EPS = 1e-6
NEG_INF = -1e30
IN_SIZES = (HA * QK, HA * QK, HA * DV, 2 * D_MLSTM, D_MLSTM, D_MLSTM, HM, HM, D_MODEL, D_MODEL)
D_IN = sum(IN_SIZES)

kernel_name = 'hybrid_diffattn_mlstm_moe_decode_step'


def rmsnorm(x, g):
    xf = x.astype(jnp.float32)
    y = xf * lax.rsqrt(jnp.mean(xf * xf, axis=-1, keepdims=True) + EPS)
    return (y * g.astype(jnp.float32)).astype(x.dtype)


def modulate(h, shift, scale):
    return h * (1.0 + scale[:, None, :]) + shift[:, None, :]


def t5_bucket(rel):
    n = jnp.maximum(rel, 0)
    max_exact = N_BUCKETS // 2
    nf = jnp.maximum(n, 1).astype(jnp.float32)
    large = max_exact + (jnp.log(nf / max_exact) / math.log(MAX_DISTANCE / max_exact) * (N_BUCKETS - max_exact)).astype(jnp.int32)
    large = jnp.minimum(large, N_BUCKETS - 1)
    return jnp.where(n < max_exact, n, large)


def rel_bias(q_pos, k_pos, rel_table):
    rel = q_pos[:, None] - k_pos[None, :]
    b = jnp.transpose(rel_table[t5_bucket(rel)], (2, 0, 1)).astype(jnp.float32)
    return jnp.where(rel[None] >= 0, b, NEG_INF)


def diff_lambda(lam_p, lam_init):
    lp = lam_p.astype(jnp.float32)
    return jnp.exp(jnp.sum(lp[0] * lp[1])) - jnp.exp(jnp.sum(lp[2] * lp[3])) + lam_init


def diff_attend(q, k, v, bias, lam):
    scale = DK ** -0.5
    s1 = jnp.einsum('bqhd,bkhd->bhqk', q[..., :DK], k[..., :DK]).astype(jnp.float32) * scale + bias
    s2 = jnp.einsum('bqhd,bkhd->bhqk', q[..., DK:], k[..., DK:]).astype(jnp.float32) * scale + bias
    w = jax.nn.softmax(s1, axis=-1) - lam * jax.nn.softmax(s2, axis=-1)
    return jnp.einsum('bhqk,bkhd->bqhd', w.astype(v.dtype), v)


def attn_prompt(q, k, v, rel_table, lam):
    B, S = q.shape[0], q.shape[1]
    k_pos = jnp.arange(S)

    def block(i):
        qs = lax.dynamic_slice_in_dim(q, i * Q_BLOCK, Q_BLOCK, axis=1)
        q_pos = i * Q_BLOCK + jnp.arange(Q_BLOCK)
        return diff_attend(qs, k, v, rel_bias(q_pos, k_pos, rel_table), lam)

    out = lax.map(block, jnp.arange(S // Q_BLOCK))
    return jnp.moveaxis(out, 0, 1).reshape(B, S, HA, DV)


def attn_sample(q, k_new, v_new, cache_k, cache_v, layer, page_table, rel_table, lam):
    Tn = q.shape[1]
    q_pos = PAST_LEN + jnp.arange(Tn)
    k_pos = jnp.arange(PAST_LEN + Tn)
    bias = rel_bias(q_pos, k_pos, rel_table)

    def one(args):
        qb, kb, vb, pages = args
        kp = cache_k[layer, pages].reshape(PAST_LEN, HA, QK)
        vp = cache_v[layer, pages].reshape(PAST_LEN, HA, DV)
        kk = jnp.concatenate([kp, kb.astype(kp.dtype)], axis=0)
        vv = jnp.concatenate([vp, vb.astype(vp.dtype)], axis=0)
        return diff_attend(qb[None], kk[None], vv[None].astype(vb.dtype), bias, lam)[0]

    return lax.map(one, (q, k_new, v_new, page_table))


def to_chunks(a, chunk):
    B, T = a.shape[0], a.shape[1]
    a = a.reshape((B, T // chunk, chunk) + a.shape[2:])
    return jnp.swapaxes(jnp.moveaxis(a, 1, 0), 2, 3)


def mlstm_chunkwise(q, k, v, ig, lf, C0, n0, m0, chunk):
    B, T = q.shape[0], q.shape[1]
    causal = jnp.tril(jnp.ones((chunk, chunk), dtype=bool))

    def step(carry, xs):
        C, n, m = carry
        qc, kc, vc, ic, fc = xs
        b = jnp.cumsum(fc, axis=-1)
        dm = jnp.where(causal, b[..., :, None] - b[..., None, :] + ic[..., None, :], -jnp.inf)
        inter = m[..., None] + b
        m_t = jnp.maximum(jnp.max(dm, axis=-1), inter)
        w_in = jnp.exp(dm - m_t[..., None])
        w_st = jnp.exp(inter - m_t)
        s = jnp.einsum('bhtd,bhsd->bhts', qc, kc) * w_in
        num = w_st[..., None] * jnp.einsum('bhvk,bhtk->bhtv', C, qc) + jnp.einsum('bhts,bhsv->bhtv', s, vc)
        den = w_st * jnp.einsum('bhk,bhtk->bht', n, qc) + jnp.sum(s, axis=-1)
        h = num / jnp.maximum(jnp.abs(den), jnp.exp(-m_t))[..., None]
        m_new = m_t[..., -1]
        decay = jnp.exp(m + b[..., -1] - m_new)
        wk = jnp.exp(b[..., -1:] - b + ic - m_new[..., None])
        C_new = decay[..., None, None] * C + jnp.einsum('bhs,bhsv,bhsk->bhvk', wk, vc, kc)
        n_new = decay[..., None] * n + jnp.einsum('bhs,bhsk->bhk', wk, kc)
        return (C_new, n_new, m_new), h

    xs = tuple(to_chunks(a, chunk) for a in (q, k, v, ig, lf))
    carry0 = (C0.astype(jnp.float32), n0.astype(jnp.float32), m0.astype(jnp.float32))
    (C, n, m), h = lax.scan(step, carry0, xs)
    h = jnp.moveaxis(jnp.swapaxes(h, 2, 3), 0, 1).reshape(B, T, HM, DH)
    return (C, n, m), h


def mlstm_branch(mqk, mv, mo, ig, fg, mstate, chunk, p):
    C0, n0, m0, conv0 = mstate
    B, T = mqk.shape[0], mqk.shape[1]
    xp = jnp.concatenate([conv0.astype(mqk.dtype), mqk], axis=1)
    w = p['w_conv']
    u = jax.nn.silu(sum(xp[:, j:j + T] * w[j] for j in range(CONV_W)) + p['b_conv'])
    new_conv = xp[:, T:]
    q = u[..., :D_MLSTM].reshape(B, T, HM, DH).astype(jnp.float32)
    k = (u[..., D_MLSTM:].reshape(B, T, HM, DH) * (DH ** -0.5)).astype(jnp.float32)
    v = mv.reshape(B, T, HM, DH).astype(jnp.float32)
    (C, n, m), h = mlstm_chunkwise(q, k, v, ig.astype(jnp.float32), jax.nn.log_sigmoid(fg.astype(jnp.float32)), C0, n0, m0, chunk)
    h = rmsnorm(h, p['g_mh']).astype(mo.dtype)
    return jax.nn.sigmoid(mo) * h.reshape(B, T, D_MLSTM), (C, n, m, new_conv)


def swiglu(t, w1, w3, w2):
    return (jax.nn.silu(t @ w1) * (t @ w3)) @ w2


def moe(h, p):
    B, T, D = h.shape
    t = h.reshape(B * T, D)
    s = jax.nn.sigmoid((t @ p['w_router']).astype(jnp.float32))
    sel = s + p['e_bias'].astype(jnp.float32)
    gscore = jnp.sum(lax.top_k(sel.reshape(-1, N_GROUPS, N_EXPERTS // N_GROUPS), 2)[0], axis=-1)
    _, gidx = lax.top_k(gscore, TOPK_GROUPS)
    gmask = jnp.any(gidx[:, :, None] == jnp.arange(N_GROUPS)[None, None, :], axis=1)
    emask = jnp.repeat(gmask, N_EXPERTS // N_GROUPS, axis=-1)
    _, eidx = lax.top_k(jnp.where(emask, sel, NEG_INF), TOP_K)
    wts = jnp.take_along_axis(s, eidx, axis=-1)
    wts = wts / jnp.sum(wts, axis=-1, keepdims=True) * ROUTED_SCALE
    gate = jnp.einsum('nk,nke->en', wts, jax.nn.one_hot(eidx, N_EXPERTS, dtype=jnp.float32))

    def add_expert(acc, xs):
        w1e, w3e, w2e, ge = xs
        return acc + ge[:, None].astype(t.dtype) * swiglu(t, w1e, w3e, w2e), None

    out, _ = lax.scan(add_expert, swiglu(t, p['ws1'], p['ws3'], p['ws2']), (p['w1'], p['w3'], p['w2'], gate))
    return out.reshape(B, T, D)


def trunk_layer(x, c, attend, mstate, chunk, lam_init, p):
    B, T = x.shape[0], x.shape[1]
    ada = jax.nn.silu(c) @ p['w_ada'] + p['b_ada']
    sh1, sc1, g1, sh2, sc2, g2 = jnp.split(ada, 6, axis=-1)
    h = modulate(rmsnorm(x, p['g_norm1']), sh1, sc1)
    z = h @ p['w_in']
    q, k, v, mqk, mv, mo, ig, fg, ga, gb = jnp.split(z, np.cumsum(IN_SIZES)[:-1].tolist(), axis=-1)
    q = q.reshape(B, T, HA, QK)
    k = k.reshape(B, T, HA, QK)
    v = v.reshape(B, T, HA, DV)
    lam = diff_lambda(p['lam'], lam_init)
    a = attend(q, k, v, lam)
    a = (rmsnorm(a, p['g_sub']) * (1.0 - lam_init)).reshape(B, T, D_ATTN)
    b_if = p['b_if']
    m_out, new_m = mlstm_branch(mqk, mv, mo, ig + b_if[:HM], fg + b_if[HM:], mstate, chunk, p)
    mix = jax.nn.sigmoid(ga) * (a @ p['w_pa']) + jax.nn.sigmoid(gb) * (m_out @ p['w_pb'])
    x = x + g1[:, None, :] * (mix @ p['w_o'])
    h2 = modulate(rmsnorm(x, p['g_norm2']), sh2, sc2)
    x = x + g2[:, None, :] * moe(h2, p)
    return x, k, v, new_m


def setup_inputs(seed: int = 0) -> dict:
    key = jax.random.key(seed)
    ks = jax.random.split(key, 48)
    cnt = [0]

    def nxt():
        kk = ks[cnt[0]]
        cnt[0] += 1
        return kk

    def nrm(shape, scale=1.0):
        r = jax.random.normal(nxt(), shape, jnp.float32)
        return r if scale == 1.0 else r * scale

    n_pages = PAST_LEN // PAGE_SIZE
    n_used = DEC_BATCH * n_pages
    n_pool = n_used + (n_used + 3) // 4
    D = D_MODEL
    inp = {}
    inp['x_prompt'] = nrm((BATCH, SEQ, D))
    inp['x_sample'] = nrm((DEC_BATCH, DEC_SEQ, D))
    inp['cache_k'] = nrm((DEPTH, n_pool, PAGE_SIZE, HA, QK))
    inp['cache_v'] = nrm((DEPTH, n_pool, PAGE_SIZE, HA, DV))
    inp['state_C'] = nrm((DEPTH, DEC_BATCH, HM, DH, DH), 0.1)
    inp['state_n'] = nrm((DEPTH, DEC_BATCH, HM, DH), 0.1)
    inp['state_m'] = nrm((DEPTH, DEC_BATCH, HM), 0.5)
    inp['state_conv'] = nrm((DEPTH, DEC_BATCH, CONV_W - 1, 2 * D_MLSTM))
    inp['page_table'] = jax.random.permutation(nxt(), n_pool)[:n_used].reshape(DEC_BATCH, n_pages).astype(jnp.int32)
    inp['c_prompt'] = nrm((BATCH, D))
    inp['c_sample'] = nrm((DEC_BATCH, D))
    inp['rel_table'] = nrm((N_BUCKETS, HA), 0.5)
    inp['w_ada'] = nrm((DEPTH, D, 6 * D), 0.5 * D ** -0.5)
    inp['b_ada'] = nrm((DEPTH, 6 * D), 0.02)
    inp['g_norm1'] = 1.0 + nrm((DEPTH, D), 0.02)
    inp['w_in'] = nrm((DEPTH, D, D_IN), D ** -0.5)
    inp['b_if'] = jnp.concatenate([nrm((DEPTH, HM), 0.1), jnp.linspace(3.0, 6.0, HM)[None, :] + nrm((DEPTH, HM), 0.1)], axis=-1)
    inp['lam_params'] = nrm((DEPTH, 4, DK), 0.1)
    inp['g_sub'] = 1.0 + nrm((DEPTH, DV), 0.02)
    inp['w_conv'] = nrm((DEPTH, CONV_W, 2 * D_MLSTM), 0.5)
    inp['b_conv'] = nrm((DEPTH, 2 * D_MLSTM), 0.02)
    inp['g_mh'] = 1.0 + nrm((DEPTH, HM, DH), 0.02)
    inp['w_pa'] = nrm((DEPTH, D_ATTN, D), D_ATTN ** -0.5)
    inp['w_pb'] = nrm((DEPTH, D_MLSTM, D), D_MLSTM ** -0.5)
    inp['w_o'] = nrm((DEPTH, D, D), D ** -0.5)
    inp['g_norm2'] = 1.0 + nrm((DEPTH, D), 0.02)
    inp['w_router'] = nrm((DEPTH, D, N_EXPERTS), D ** -0.5)
    inp['e_bias'] = nrm((DEPTH, N_EXPERTS), 0.01)
    inp['w1'] = nrm((DEPTH, N_EXPERTS, D, D_EXPERT), D ** -0.5)
    inp['w3'] = nrm((DEPTH, N_EXPERTS, D, D_EXPERT), D ** -0.5)
    inp['w2'] = nrm((DEPTH, N_EXPERTS, D_EXPERT, D), D_EXPERT ** -0.5)
    inp['ws1'] = nrm((DEPTH, D, D_SHARED), D ** -0.5)
    inp['ws3'] = nrm((DEPTH, D, D_SHARED), D ** -0.5)
    inp['ws2'] = nrm((DEPTH, D_SHARED, D), D_SHARED ** -0.5)
    inp['g_final'] = 1.0 + nrm((D,), 0.02)
    return inp


def reference(x_prompt, x_sample, cache_k, cache_v, state_C, state_n, state_m, state_conv, page_table, c_prompt, c_sample, rel_table, w_ada, b_ada, g_norm1, w_in, b_if, lam_params, g_sub, w_conv, b_conv, g_mh, w_pa, w_pb, w_o, g_norm2, w_router, e_bias, w1, w3, w2, ws1, ws3, ws2, g_final):
    xp, xs = x_prompt, x_sample
    kp_l, vp_l, ks_l, vs_l = [], [], [], []
    cp_l, np_l, mp_l, convp_l = [], [], [], []
    cs_l, ns_l, ms_l, convs_l = [], [], [], []
    for l in range(DEPTH):
        p = dict(w_ada=w_ada[l], b_ada=b_ada[l], g_norm1=g_norm1[l], w_in=w_in[l], b_if=b_if[l], lam=lam_params[l],
                 g_sub=g_sub[l], w_conv=w_conv[l], b_conv=b_conv[l], g_mh=g_mh[l], w_pa=w_pa[l], w_pb=w_pb[l],
                 w_o=w_o[l], g_norm2=g_norm2[l], w_router=w_router[l], e_bias=e_bias[l], w1=w1[l], w3=w3[l],
                 w2=w2[l], ws1=ws1[l], ws3=ws3[l], ws2=ws2[l])
        lam_init = 0.8 - 0.6 * math.exp(-0.3 * l)
        B = xp.shape[0]
        mstate_p = (jnp.zeros((B, HM, DH, DH), jnp.float32), jnp.zeros((B, HM, DH), jnp.float32),
                    jnp.zeros((B, HM), jnp.float32), jnp.zeros((B, CONV_W - 1, 2 * D_MLSTM), xp.dtype))
        attend_p = lambda q, k, v, lam: attn_prompt(q, k, v, rel_table, lam)
        xp, kp, vp, (Cp, npv, mp, convp) = trunk_layer(xp, c_prompt, attend_p, mstate_p, MLSTM_CHUNK, lam_init, p)
        mstate_s = (state_C[l], state_n[l], state_m[l], state_conv[l])
        attend_s = lambda q, k, v, lam, l=l: attn_sample(q, k, v, cache_k, cache_v, l, page_table, rel_table, lam)
        xs, ksn, vsn, (Cs, nsv, ms, convs) = trunk_layer(xs, c_sample, attend_s, mstate_s, xs.shape[1], lam_init, p)
        kp_l.append(kp); vp_l.append(vp); ks_l.append(ksn); vs_l.append(vsn)
        cp_l.append(Cp.astype(state_C.dtype)); np_l.append(npv.astype(state_n.dtype))
        mp_l.append(mp.astype(state_m.dtype)); convp_l.append(convp.astype(state_conv.dtype))
        cs_l.append(Cs.astype(state_C.dtype)); ns_l.append(nsv.astype(state_n.dtype))
        ms_l.append(ms.astype(state_m.dtype)); convs_l.append(convs.astype(state_conv.dtype))
    y_prompt = rmsnorm(xp, g_final)
    y_sample = rmsnorm(xs, g_final)
    return (y_prompt, y_sample, jnp.stack(kp_l), jnp.stack(vp_l), jnp.stack(ks_l), jnp.stack(vs_l),
            jnp.stack(cp_l), jnp.stack(np_l), jnp.stack(mp_l), jnp.stack(convp_l),
            jnp.stack(cs_l), jnp.stack(ns_l), jnp.stack(ms_l), jnp.stack(convs_l))
```

```python
import functools
import math

import numpy as np
import jax
import jax.numpy as jnp
from jax import lax
from jax.experimental import pallas as pl
from jax.experimental.pallas import tpu as pltpu

F32 = jnp.float32
BF16 = jnp.bfloat16

HA = 8
DK = 64
DV = 128
HM = 4
DH = 256
CONV_W = 4
MLSTM_CHUNK = 128
N_BUCKETS = 32
MAX_DISTANCE = 128
N_EXPERTS = 64
TOP_K = 8
N_GROUPS = 8
TOPK_GROUPS = 4
ROUTED_SCALE = 2.5
EPS = 1e-6
NEG_INF = -1e30
PAGE_SIZE = 128

V7X_VMEM_BYTES = 64 * 1024 * 1024
VMEM_LIMIT = V7X_VMEM_BYTES - 12 * 1024 * 1024
LANES = 128
SUBLANES = 8

ATTN_BLOCK = 512
PAGES_PER_STEP = 4


def _params(sem):
    return pltpu.CompilerParams(dimension_semantics=sem, vmem_limit_bytes=VMEM_LIMIT)


def _sigmoid(x):
    return 1.0 / (1.0 + jnp.exp(-x))


def _log_sigmoid(x):
    return jnp.minimum(x, 0.0) - jnp.log1p(jnp.exp(-jnp.abs(x)))


def _pick_tile(n, pref):
    t = min(n, pref)
    while n % t:
        t //= 2
    return t


def _ada_kernel(c_ref, w_ref, b_ref, o_ref):
    c = c_ref[...]
    a = (c * _sigmoid(c)).astype(BF16)
    o_ref[...] = jnp.dot(a, w_ref[...].astype(BF16), preferred_element_type=F32) + b_ref[...]


def _ada(c, w, b):
    rows, d = c.shape
    r = -(-rows // SUBLANES) * SUBLANES
    c = jnp.pad(c, ((0, r - rows), (0, 0)))
    n = w.shape[1]
    tn = _pick_tile(n, 768)
    return pl.pallas_call(
        _ada_kernel,
        out_shape=jax.ShapeDtypeStruct((r, n), F32),
        grid=(n // tn,),
        in_specs=[pl.BlockSpec((r, d), lambda j: (0, 0)),
                  pl.BlockSpec((d, tn), lambda j: (0, j)),
                  pl.BlockSpec((1, tn), lambda j: (0, j))],
        out_specs=pl.BlockSpec((r, tn), lambda j: (0, j)),
        compiler_params=_params(("arbitrary",)),
        name="ada",
    )(c, w, b.reshape(1, n))[:rows]


def _normmod_kernel(x_ref, g_ref, sh_ref, sc_ref, o_ref):
    x = x_ref[...]
    y = x * lax.rsqrt(jnp.mean(x * x, axis=-1, keepdims=True) + EPS) * g_ref[...]
    o_ref[...] = (y * (1.0 + sc_ref[...]) + sh_ref[...]).astype(o_ref.dtype)


def _normmod(x, g, sh, sc):
    b, s, d = x.shape
    r = sh.shape[1]
    ts = _pick_tile(s, 1024)
    rb = 1 if r == 1 else ts
    mod_map = (lambda bi, i: (bi, 0, 0)) if r == 1 else (lambda bi, i: (bi, i, 0))
    return pl.pallas_call(
        _normmod_kernel,
        out_shape=jax.ShapeDtypeStruct((b, s, d), BF16),
        grid=(b, s // ts),
        in_specs=[pl.BlockSpec((None, ts, d), lambda bi, i: (bi, i, 0)),
                  pl.BlockSpec((1, d), lambda bi, i: (0, 0)),
                  pl.BlockSpec((None, rb, d), mod_map),
                  pl.BlockSpec((None, rb, d), mod_map)],
        out_specs=pl.BlockSpec((None, ts, d), lambda bi, i: (bi, i, 0)),
        compiler_params=_params(("arbitrary", "arbitrary")),
        name="normmod",
    )(x, g.reshape(1, d), sh, sc)


def _mm_kernel(x_ref, w_ref, *o_refs, scale):
    acc = jnp.dot(x_ref[...], w_ref[...], preferred_element_type=F32)
    if scale != 1.0:
        acc = acc * scale
    for o in o_refs:
        o[...] = acc.astype(o.dtype)


def _mm(x, w, out_dtypes, scale=1.0):
    m, k = x.shape
    n = w.shape[1]
    tm = _pick_tile(m, 1024)
    tn = _pick_tile(n, 1024)
    outs = pl.pallas_call(
        functools.partial(_mm_kernel, scale=scale),
        out_shape=[jax.ShapeDtypeStruct((m, n), dt) for dt in out_dtypes],
        grid=(m // tm, n // tn),
        in_specs=[pl.BlockSpec((tm, k), lambda i, j: (i, 0)),
                  pl.BlockSpec((k, tn), lambda i, j: (0, j))],
        out_specs=[pl.BlockSpec((tm, tn), lambda i, j: (i, j)) for _ in out_dtypes],
        compiler_params=_params(("arbitrary", "arbitrary")),
        name="mm",
    )(x, w)
    return outs


def _t5_bucket_table(max_rel):
    n = np.arange(max_rel + 1)
    max_exact = N_BUCKETS // 2
    nf = np.maximum(n, 1).astype(np.float32)
    large = max_exact + (np.log(nf / np.float32(max_exact)) / np.float32(math.log(MAX_DISTANCE / max_exact))
                         * np.float32(N_BUCKETS - max_exact)).astype(np.int32)
    large = np.minimum(large, N_BUCKETS - 1)
    return np.where(n < max_exact, n, large).astype(np.int32)


def _far_bucket_start():
    tab = _t5_bucket_table(4 * MAX_DISTANCE)
    far = int(np.max(np.nonzero(tab != N_BUCKETS - 1)[0])) + 1
    return far


def _shifted_table(rel_table):
    return rel_table.astype(F32) - rel_table[N_BUCKETS - 1][None, :].astype(F32)


def _prompt_bias_tiles(rel_table, t):
    assert t >= _far_bucket_start()
    tab = _shifted_table(rel_table)
    kv = np.arange(t)[:, None]
    q = np.arange(t)[None, :]
    bt = _t5_bucket_table(2 * t)
    rel0 = q - kv
    diag = jnp.where(jnp.asarray(rel0 >= 0)[:, :, None], tab[bt[np.maximum(rel0, 0)]], NEG_INF)
    sub = tab[bt[q + t - kv]]
    tiles = jnp.stack([diag, sub, jnp.zeros_like(sub)], axis=0)
    return jnp.transpose(tiles, (3, 0, 1, 2))


def _attn_kernel(qi_ref, kj_ref, kind_ref, lam_ref, q_ref, k_ref, vt_ref, bias_ref, gs_ref,
                 o_ref, m_sc, l_sc, acc_sc, *, tq, lam_init):
    del kind_ref
    t = pl.program_id(2)
    qi = qi_ref[t]
    kj = kj_ref[t]

    @pl.when(kj == 0)
    def _():
        m_sc[...] = jnp.full(m_sc.shape, NEG_INF, F32)
        l_sc[...] = jnp.zeros(l_sc.shape, F32)
        acc_sc[...] = jnp.zeros(acc_sc.shape, F32)

    q = q_ref[...].reshape(2 * tq, q_ref.shape[-1])
    st = pl.dot(k_ref[...], q, trans_b=True)
    b = bias_ref[...]
    st = st + jnp.concatenate([b, b], axis=1)
    m_old = m_sc[...]
    m_new = jnp.maximum(m_old, jnp.max(st, axis=0, keepdims=True))
    alpha = jnp.exp(m_old - m_new)
    p = jnp.exp(st - m_new)
    l_sc[...] = alpha * l_sc[...] + jnp.sum(p, axis=0, keepdims=True)
    acc_sc[...] = alpha * acc_sc[...] + jnp.dot(vt_ref[...], p.astype(BF16), preferred_element_type=F32)
    m_sc[...] = m_new

    @pl.when(kj == qi)
    def _():
        o = acc_sc[...] * (1.0 / l_sc[...])
        a = o[:, :tq] - lam_ref[0] * o[:, tq:]
        a = a * lax.rsqrt(jnp.mean(a * a, axis=0, keepdims=True) + EPS) * gs_ref[...] * (1.0 - lam_init)
        o_ref[...] = a.T.astype(o_ref.dtype)


def _attn_prompt(qbd, k_bf, vt, bias_tiles, lam, g_sub, lam_init):
    b, _, _, s, _ = qbd.shape
    t = ATTN_BLOCK
    nq = s // t
    qi_l, kj_l, kind_l = [], [], []
    for i in range(nq):
        for j in range(i + 1):
            qi_l.append(i)
            kj_l.append(j)
            kind_l.append(0 if j == i else (1 if j == i - 1 else 2))
    qi_t = jnp.asarray(np.array(qi_l, np.int32))
    kj_t = jnp.asarray(np.array(kj_l, np.int32))
    kind_t = jnp.asarray(np.array(kind_l, np.int32))
    grid_spec = pltpu.PrefetchScalarGridSpec(
        num_scalar_prefetch=3,
        grid=(b, HA, len(qi_l)),
        in_specs=[
            pl.BlockSpec(memory_space=pltpu.SMEM),
            pl.BlockSpec((None, None, 2, t, 2 * DK), lambda bi, h, st, qi, kj, kd: (bi, h, 0, qi[st], 0)),
            pl.BlockSpec((None, t, 2 * DK), lambda bi, h, st, qi, kj, kd: (bi, kj[st], h)),
            pl.BlockSpec((None, None, DV, t), lambda bi, h, st, qi, kj, kd: (bi, h, 0, kj[st])),
            pl.BlockSpec((None, None, t, t), lambda bi, h, st, qi, kj, kd: (h, kd[st], 0, 0)),
            pl.BlockSpec((DV, 1), lambda bi, h, st, qi, kj, kd: (0, 0)),
        ],
        out_specs=pl.BlockSpec((None, t, DV), lambda bi, h, st, qi, kj, kd: (bi, qi[st], h)),
        scratch_shapes=[pltpu.VMEM((1, 2 * t), F32), pltpu.VMEM((1, 2 * t), F32), pltpu.VMEM((DV, 2 * t), F32)],
    )
    return pl.pallas_call(
        functools.partial(_attn_kernel, tq=t, lam_init=lam_init),
        out_shape=jax.ShapeDtypeStruct((b, s, HA * DV), BF16),
        grid_spec=grid_spec,
        compiler_params=_params(("arbitrary", "arbitrary", "arbitrary")),
        name="attn_prompt",
    )(qi_t, kj_t, kind_t, lam, qbd, k_bf, vt, bias_tiles, g_sub.reshape(DV, 1))


def _attn_sample_kernel(pt_ref, lam_ref, q_ref, kn_ref, vn_ref, bias_ref, bnew_ref, gs_ref, *refs, pps, lam_init):
    del pt_ref
    k_refs = refs[:pps]
    v_refs = refs[pps:2 * pps]
    o_ref = refs[2 * pps]
    m_sc, l_sc, acc_sc = refs[2 * pps + 1:]
    p_id = pl.program_id(1)
    n_steps = pl.num_programs(1)

    @pl.when(p_id == 0)
    def _():
        m_sc[...] = jnp.full(m_sc.shape, NEG_INF, F32)
        l_sc[...] = jnp.zeros(l_sc.shape, F32)
        acc_sc[...] = jnp.zeros(acc_sc.shape, F32)

    q = q_ref[...]
    for r in range(pps):
        kb = k_refs[r][...].astype(BF16)
        vb = v_refs[r][...].astype(BF16)
        st = pl.dot(q, kb, trans_b=True) + bias_ref[r]
        m_old = m_sc[...]
        m_new = jnp.maximum(m_old, jnp.max(st, axis=1, keepdims=True))
        alpha = jnp.exp(m_old - m_new)
        p = jnp.exp(st - m_new)
        l_sc[...] = alpha * l_sc[...] + jnp.sum(p, axis=1, keepdims=True)
        acc_sc[...] = alpha * acc_sc[...] + jnp.dot(p.astype(BF16), vb, preferred_element_type=F32)
        m_sc[...] = m_new

    @pl.when(p_id == n_steps - 1)
    def _():
        kn = kn_ref[...].astype(BF16).astype(F32)
        vn = vn_ref[...].astype(BF16).astype(F32)
        s_new = jnp.sum(q.astype(F32) * kn, axis=1, keepdims=True) + bnew_ref[...]
        m_old = m_sc[...]
        m_new = jnp.maximum(m_old, s_new)
        alpha = jnp.exp(m_old - m_new)
        p_new = jnp.exp(s_new - m_new)
        l = alpha * l_sc[...] + p_new
        acc = alpha * acc_sc[...] + p_new * vn
        o = acc * (1.0 / l)
        lam = lam_ref[0]
        for h in range(HA):
            o1 = o[2 * h:2 * h + 1, h * DV:(h + 1) * DV]
            o2 = o[2 * h + 1:2 * h + 2, h * DV:(h + 1) * DV]
            a = o1 - lam * o2
            a = a * lax.rsqrt(jnp.mean(a * a, axis=1, keepdims=True) + EPS) * gs_ref[...] * (1.0 - lam_init)
            o_ref[:, h * DV:(h + 1) * DV] = a.astype(o_ref.dtype)


def _attn_sample(qmat, k_new, v_new, cache_k, cache_v, page_table, bias_pages, bias_new, lam, g_sub, lam_init):
    b, n_pages = page_table.shape
    pps = PAGES_PER_STEP
    while n_pages % pps:
        pps //= 2
    d = HA * 2 * DK
    n_steps = n_pages // pps

    def page_map(r):
        return lambda bi, p, pt: (pt[bi, p * pps + r], 0, 0)

    in_specs = [
        pl.BlockSpec(memory_space=pltpu.SMEM),
        pl.BlockSpec((None, 2 * HA, d), lambda bi, p, pt: (bi, 0, 0)),
        pl.BlockSpec((None, 1, d), lambda bi, p, pt: (bi, 0, 0)),
        pl.BlockSpec((None, 1, d), lambda bi, p, pt: (bi, 0, 0)),
        pl.BlockSpec((pps, 2 * HA, PAGE_SIZE), lambda bi, p, pt: (p, 0, 0)),
        pl.BlockSpec((2 * HA, 1), lambda bi, p, pt: (0, 0)),
        pl.BlockSpec((1, DV), lambda bi, p, pt: (0, 0)),
    ]
    in_specs += [pl.BlockSpec((None, PAGE_SIZE, d), page_map(r)) for r in range(pps)]
    in_specs += [pl.BlockSpec((None, PAGE_SIZE, d), page_map(r)) for r in range(pps)]
    grid_spec = pltpu.PrefetchScalarGridSpec(
        num_scalar_prefetch=1,
        grid=(b, n_steps),
        in_specs=in_specs,
        out_specs=pl.BlockSpec((None, 1, HA * DV), lambda bi, p, pt: (bi, 0, 0)),
        scratch_shapes=[pltpu.VMEM((2 * HA, 1), F32), pltpu.VMEM((2 * HA, 1), F32), pltpu.VMEM((2 * HA, d), F32)],
    )
    return pl.pallas_call(
        functools.partial(_attn_sample_kernel, pps=pps, lam_init=lam_init),
        out_shape=jax.ShapeDtypeStruct((b, 1, HA * DV), BF16),
        grid_spec=grid_spec,
        compiler_params=_params(("arbitrary", "arbitrary")),
        name="attn_sample",
    )(page_table, lam, qmat, k_new, v_new, bias_pages, bias_new, g_sub.reshape(1, DV),
      *([cache_k] * pps), *([cache_v] * pps))


def _mlstm_kernel(mqk_ref, mv_ref, mo_ref, grow_ref, gcol_ref, wconv_ref, bconv_ref, bifc_ref, bifr_ref, gmh_ref,
                  o_ref, c_ref, n_ref, m_ref, xs_sc):
    c_id = pl.program_id(1)
    L = MLSTM_CHUNK
    dm_ = HM * DH

    @pl.when(c_id == 0)
    def _():
        xs_sc[0:SUBLANES, :] = jnp.zeros((SUBLANES, xs_sc.shape[1]), F32)
        c_ref[...] = jnp.zeros(c_ref.shape, F32)
        n_ref[...] = jnp.zeros(n_ref.shape, F32)
        m_ref[...] = jnp.zeros(m_ref.shape, F32)

    xs_sc[SUBLANES:SUBLANES + L, :] = mqk_ref[...]
    w = wconv_ref[...]
    u = bconv_ref[...]
    for j in range(CONV_W):
        start = SUBLANES - (CONV_W - 1) + j
        u = u + xs_sc[start:start + L, :] * w[j:j + 1, :]
    u = u * _sigmoid(u)
    xs_sc[0:SUBLANES, :] = xs_sc[L:L + SUBLANES, :]

    gr = grow_ref[...] + bifc_ref[...]
    gc = gcol_ref[...] + bifr_ref[...]
    lfr = _log_sigmoid(gr)
    lfc = _log_sigmoid(gc)
    ri = lax.broadcasted_iota(jnp.int32, (L, L), 0)
    ci = lax.broadcasted_iota(jnp.int32, (L, L), 1)
    upper = (ri <= ci).astype(F32)
    lower = (ci <= ri).astype(F32)
    causal = ci <= ri
    b_r = jnp.dot(lfr, upper, preferred_element_type=F32, precision=lax.Precision.HIGHEST)
    b_c = jnp.dot(lower, lfc, preferred_element_type=F32, precision=lax.Precision.HIGHEST)

    for h in range(HM):
        sl = slice(h * DH, (h + 1) * DH)
        b_col = b_c[:, HM + h:HM + h + 1]
        b_row = b_r[HM + h:HM + h + 1, :]
        ig_row = gr[h:h + 1, :]
        ig_col = gc[:, h:h + 1]
        m_prev = m_ref[h:h + 1, 0:1]
        dmat = jnp.where(causal, b_col - b_row + ig_row, -jnp.inf)
        inter = m_prev + b_col
        m_t = jnp.maximum(jnp.max(dmat, axis=1, keepdims=True), inter)
        w_in = jnp.exp(dmat - m_t)
        w_st = jnp.exp(inter - m_t)
        q_f = u[:, sl]
        k_f = u[:, dm_ + h * DH:dm_ + (h + 1) * DH] * (DH ** -0.5)
        q_b = q_f.astype(BF16)
        k_b = k_f.astype(BF16)
        v_b = mv_ref[:, sl]
        c_old = c_ref[h]
        n_old = n_ref[h:h + 1, :]
        s = pl.dot(q_b, k_b, trans_b=True) * w_in
        num = w_st * pl.dot(q_b, c_old.astype(BF16), trans_b=True) + jnp.dot(s.astype(BF16), v_b,
                                                                              preferred_element_type=F32)
        den = w_st * jnp.sum(q_f * n_old, axis=1, keepdims=True) + jnp.sum(s, axis=1, keepdims=True)
        hh = num / jnp.maximum(jnp.abs(den), jnp.exp(-m_t))
        hn = hh * lax.rsqrt(jnp.mean(hh * hh, axis=1, keepdims=True) + EPS) * gmh_ref[:, sl]
        o_ref[:, sl] = (_sigmoid(mo_ref[:, sl]) * hn).astype(o_ref.dtype)

        m_new = m_t[L - 1:L, :]
        b_last = b_row[:, L - 1:L]
        decay = jnp.exp(m_prev + b_last - m_new)
        wk_col = jnp.exp(b_last - b_col + ig_col - m_new)
        wv_t = (wk_col * v_b.astype(F32)).T.astype(BF16)
        c_ref[h] = decay * c_old + jnp.dot(wv_t, k_b, preferred_element_type=F32)
        n_ref[h:h + 1, :] = decay * n_old + jnp.sum(wk_col * k_f, axis=0, keepdims=True)
        m_ref[h:h + 1, :] = jnp.broadcast_to(m_new, (1, m_ref.shape[1]))


def _mlstm_prompt(mqk, mv_bf, mo, g_rows, g_cols, w_conv, b_conv, b_if, g_mh):
    b, s, d2 = mqk.shape
    d = d2 // 2
    L = MLSTM_CHUNK
    nc = s // L
    bif_col = b_if.reshape(2 * HM, 1)
    bif_row = jnp.zeros((1, LANES), F32).at[0, :2 * HM].set(b_if)
    out_shapes = [jax.ShapeDtypeStruct((b, s, d), BF16),
                  jax.ShapeDtypeStruct((b, HM, DH, DH), F32),
                  jax.ShapeDtypeStruct((b, HM, DH), F32),
                  jax.ShapeDtypeStruct((b, SUBLANES, LANES), F32)]
    return pl.pallas_call(
        _mlstm_kernel,
        out_shape=out_shapes,
        grid=(b, nc),
        in_specs=[pl.BlockSpec((None, L, d2), lambda bi, c: (bi, c, 0)),
                  pl.BlockSpec((None, L, d), lambda bi, c: (bi, c, 0)),
                  pl.BlockSpec((None, L, d), lambda bi, c: (bi, c, 0)),
                  pl.BlockSpec((None, None, 2 * HM, L), lambda bi, c: (bi, c, 0, 0)),
                  pl.BlockSpec((None, L, LANES), lambda bi, c: (bi, c, 0)),
                  pl.BlockSpec((CONV_W, d2), lambda bi, c: (0, 0)),
                  pl.BlockSpec((1, d2), lambda bi, c: (0, 0)),
                  pl.BlockSpec((2 * HM, 1), lambda bi, c: (0, 0)),
                  pl.BlockSpec((1, LANES), lambda bi, c: (0, 0)),
                  pl.BlockSpec((1, d), lambda bi, c: (0, 0))],
        out_specs=[pl.BlockSpec((None, L, d), lambda bi, c: (bi, c, 0)),
                   pl.BlockSpec((None, HM, DH, DH), lambda bi, c: (bi, 0, 0, 0)),
                   pl.BlockSpec((None, HM, DH), lambda bi, c: (bi, 0, 0)),
                   pl.BlockSpec((None, SUBLANES, LANES), lambda bi, c: (bi, 0, 0))],
        scratch_shapes=[pltpu.VMEM((L + 2 * SUBLANES, d2), F32)],
        compiler_params=_params(("arbitrary", "arbitrary")),
        name="mlstm_prompt",
    )(mqk, mv_bf, mo, g_rows, g_cols, w_conv, b_conv.reshape(1, d2), bif_col, bif_row, g_mh.reshape(1, d))


def _mlstm_step_kernel(mqk_ref, conv_ref, mv_ref, mo_ref, g_ref, bif_ref, c0_ref, n0_ref, m0_ref,
                       wconv_ref, bconv_ref, gmh_ref, o_ref, c_ref, n_ref, m_ref, convo_ref):
    dm_ = HM * DH
    w = wconv_ref[...]
    x_new = mqk_ref[...]
    u = bconv_ref[...] + x_new * w[CONV_W - 1:CONV_W, :]
    for j in range(CONV_W - 1):
        u = u + conv_ref[j:j + 1, :] * w[j:j + 1, :]
    u = u * _sigmoid(u)
    convo_ref[0:CONV_W - 2, :] = conv_ref[1:CONV_W - 1, :]
    convo_ref[CONV_W - 2:CONV_W - 1, :] = x_new

    g = g_ref[...] + bif_ref[...]
    lf_all = _log_sigmoid(g)
    for h in range(HM):
        sl = slice(h * DH, (h + 1) * DH)
        ig = g[:, h:h + 1]
        lf = lf_all[:, HM + h:HM + h + 1]
        m_prev = m0_ref[:, h:h + 1]
        inter = m_prev + lf
        m_t = jnp.maximum(ig, inter)
        w_in = jnp.exp(ig - m_t)
        w_st = jnp.exp(inter - m_t)
        q_f = u[:, sl]
        k_f = u[:, dm_ + h * DH:dm_ + (h + 1) * DH] * (DH ** -0.5)
        q_r = q_f.astype(BF16)
        k_r = k_f.astype(BF16)
        v_f = mv_ref[:, sl].astype(F32)
        c_old = c0_ref[h]
        n_old = n0_ref[h:h + 1, :]
        s = jnp.sum(q_r.astype(F32) * k_r.astype(F32), axis=1, keepdims=True) * w_in
        q8 = jnp.broadcast_to(q_r, (SUBLANES, DH))
        cq = pl.dot(q8, c_old.astype(BF16), trans_b=True)[0:1, :]
        num = w_st * cq + s * v_f
        den = w_st * jnp.sum(q_f * n_old, axis=1, keepdims=True) + s
        hh = num / jnp.maximum(jnp.abs(den), jnp.exp(-m_t))
        hn = hh * lax.rsqrt(jnp.mean(hh * hh, axis=1, keepdims=True) + EPS) * gmh_ref[:, sl]
        o_ref[:, sl] = (_sigmoid(mo_ref[:, sl]) * hn).astype(o_ref.dtype)

        v_rows = jnp.broadcast_to(v_f, (LANES, DH)).T
        v_cols = jnp.concatenate([v_rows] * (DH // LANES), axis=1)
        c_ref[h] = w_st * c_old + w_in * (v_cols * k_f)
        n_ref[h:h + 1, :] = w_st * n_old + w_in * k_f
        m_ref[:, h:h + 1] = m_t


def _mlstm_sample(mqk, conv0, mv_bf, mo, gates, b_if, c0, n0, m0, w_conv, b_conv, g_mh):
    b = mqk.shape[0]
    d2 = mqk.shape[-1]
    d = d2 // 2
    bif_row = jnp.zeros((1, LANES), F32).at[0, :2 * HM].set(b_if)
    out_shapes = [jax.ShapeDtypeStruct((b, 1, d), BF16),
                  jax.ShapeDtypeStruct((b, HM, DH, DH), F32),
                  jax.ShapeDtypeStruct((b, HM, DH), F32),
                  jax.ShapeDtypeStruct((b, 1, HM), F32),
                  jax.ShapeDtypeStruct((b, CONV_W - 1, d2), F32)]
    row = lambda bi: (bi, 0, 0)
    fixed = lambda bi: (0, 0)
    return pl.pallas_call(
        _mlstm_step_kernel,
        out_shape=out_shapes,
        grid=(b,),
        in_specs=[pl.BlockSpec((None, 1, d2), row),
                  pl.BlockSpec((None, CONV_W - 1, d2), row),
                  pl.BlockSpec((None, 1, d), row),
                  pl.BlockSpec((None, 1, d), row),
                  pl.BlockSpec((None, 1, LANES), row),
                  pl.BlockSpec((1, LANES), fixed),
                  pl.BlockSpec((None, HM, DH, DH), lambda bi: (bi, 0, 0, 0)),
                  pl.BlockSpec((None, HM, DH), row),
                  pl.BlockSpec((None, 1, HM), row),
                  pl.BlockSpec((CONV_W, d2), fixed),
                  pl.BlockSpec((1, d2), fixed),
                  pl.BlockSpec((1, d), fixed)],
        out_specs=[pl.BlockSpec((None, 1, d), row),
                   pl.BlockSpec((None, HM, DH, DH), lambda bi: (bi, 0, 0, 0)),
                   pl.BlockSpec((None, HM, DH), row),
                   pl.BlockSpec((None, 1, HM), row),
                   pl.BlockSpec((None, CONV_W - 1, d2), row)],
        compiler_params=_params(("arbitrary",)),
        name="mlstm_sample",
    )(mqk, conv0, mv_bf, mo, gates, bif_row, c0, n0, m0, w_conv, b_conv.reshape(1, d2), g_mh.reshape(1, d))


def _route(logits_t, ebias_col):
    e, t = logits_t.shape
    gsz = e // N_GROUPS
    s = _sigmoid(logits_t)
    sel = s + ebias_col
    neg = -jnp.inf
    member = lax.broadcasted_iota(jnp.int32, (gsz, t), 0).astype(F32)
    gscore = []
    for g in range(N_GROUPS):
        blk = sel[g * gsz:(g + 1) * gsz, :]
        top1 = jnp.max(blk, axis=0, keepdims=True)
        first = jnp.min(jnp.where(blk == top1, member, float(gsz)), axis=0, keepdims=True)
        top2 = jnp.max(jnp.where(member == first, neg, blk), axis=0, keepdims=True)
        gscore.append(top1 + top2)
    blocks = []
    for g in range(N_GROUPS):
        rank = jnp.zeros((1, t), F32)
        for g2 in range(N_GROUPS):
            if g2 == g:
                continue
            ahead = (gscore[g2] > gscore[g]) if g2 > g else (gscore[g2] >= gscore[g])
            rank = rank + jnp.where(ahead, 1.0, 0.0)
        blocks.append(jnp.where(rank < float(TOPK_GROUPS), sel[g * gsz:(g + 1) * gsz, :], NEG_INF))
    v = jnp.concatenate(blocks, axis=0)
    eiota = lax.broadcasted_iota(jnp.int32, (e, t), 0).astype(F32)
    wts = jnp.zeros((e, t), F32)
    for _ in range(TOP_K):
        top = jnp.max(v, axis=0, keepdims=True)
        idx = jnp.min(jnp.where(v == top, eiota, float(e)), axis=0, keepdims=True)
        hit = eiota == idx
        wts = jnp.where(hit, s, wts)
        v = jnp.where(hit, neg, v)
    wts = wts / jnp.sum(wts, axis=0, keepdims=True) * ROUTED_SCALE
    pad = jnp.where(lax.broadcasted_iota(jnp.int32, (LANES - e, t), 0) == 0, 1.0, 0.0).astype(F32)
    return jnp.concatenate([wts, pad], axis=0)


def _mix_kernel(a_ref, mo_ref, ga_ref, gb_ref, x_ref, g1_ref, wpa_ref, wpb_ref, wo_ref,
                gn2_ref, sh2_ref, sc2_ref, wr_ref, eb_ref, x1_ref, h2_ref, gate_ref):
    pa = jnp.dot(a_ref[...], wpa_ref[...], preferred_element_type=F32)
    pb = jnp.dot(mo_ref[...], wpb_ref[...], preferred_element_type=F32)
    mix = _sigmoid(ga_ref[...]) * pa + _sigmoid(gb_ref[...]) * pb
    x1 = x_ref[...] + g1_ref[...] * jnp.dot(mix.astype(BF16), wo_ref[...], preferred_element_type=F32)
    x1_ref[...] = x1
    y = x1 * lax.rsqrt(jnp.mean(x1 * x1, axis=-1, keepdims=True) + EPS) * gn2_ref[...]
    h2 = y * (1.0 + sc2_ref[...]) + sh2_ref[...]
    h2_ref[...] = h2.astype(h2_ref.dtype)
    logits_t = pl.dot(wr_ref[...], h2, trans_b=True, precision=lax.Precision.HIGHEST)
    gate_ref[...] = _route(logits_t, eb_ref[...]).T


def _mix(a_bf, mo_bf, ga, gb, x, g1, sh2, sc2, w_pa, w_pb, w_o, g_norm2, w_router_t, e_bias):
    b, s, d = x.shape
    r = g1.shape[1]
    ts = _pick_tile(s, 512)
    rb = 1 if r == 1 else ts
    tok = lambda bi, i: (bi, i, 0)
    mod = (lambda bi, i: (bi, 0, 0)) if r == 1 else tok
    fixed = lambda bi, i: (0, 0)
    e = w_router_t.shape[0]
    return pl.pallas_call(
        _mix_kernel,
        out_shape=[jax.ShapeDtypeStruct((b, s, d), F32),
                   jax.ShapeDtypeStruct((b, s, d), BF16),
                   jax.ShapeDtypeStruct((b, s, LANES), F32)],
        grid=(b, s // ts),
        in_specs=[pl.BlockSpec((None, ts, d), tok),
                  pl.BlockSpec((None, ts, d), tok),
                  pl.BlockSpec((None, ts, d), tok),
                  pl.BlockSpec((None, ts, d), tok),
                  pl.BlockSpec((None, ts, d), tok),
                  pl.BlockSpec((None, rb, d), mod),
                  pl.BlockSpec((d, d), fixed),
                  pl.BlockSpec((d, d), fixed),
                  pl.BlockSpec((d, d), fixed),
                  pl.BlockSpec((1, d), fixed),
                  pl.BlockSpec((None, rb, d), mod),
                  pl.BlockSpec((None, rb, d), mod),
                  pl.BlockSpec((e, d), fixed),
                  pl.BlockSpec((e, 1), fixed)],
        out_specs=[pl.BlockSpec((None, ts, d), tok),
                   pl.BlockSpec((None, ts, d), tok),
                   pl.BlockSpec((None, ts, LANES), tok)],
        compiler_params=_params(("arbitrary", "arbitrary")),
        name="mix_route",
    )(a_bf, mo_bf, ga, gb, x, g1, w_pa, w_pb, w_o, g_norm2.reshape(1, d), sh2, sc2, w_router_t,
      e_bias.reshape(e, 1))


def _moe_kernel(h_ref, gate_ref, w1_ref, w3_ref, w2_ref, x1_ref, g2_ref, gf_ref, y_ref, acc_sc, *, final_norm):
    e = pl.program_id(2)
    n_e = pl.num_programs(2)

    @pl.when(e == 0)
    def _():
        acc_sc[...] = jnp.zeros(acc_sc.shape, F32)

    h = h_ref[...]
    h1 = jnp.dot(h, w1_ref[...], preferred_element_type=F32)
    h3 = jnp.dot(h, w3_ref[...], preferred_element_type=F32)
    act = (h1 * _sigmoid(h1) * h3).astype(BF16)
    ye = jnp.dot(act, w2_ref[...], preferred_element_type=F32)
    gate = gate_ref[...]
    lane = lax.broadcasted_iota(jnp.int32, gate.shape, 1)
    g_col = jnp.sum(jnp.where(lane == e, gate, 0.0), axis=1, keepdims=True)
    acc_sc[...] += g_col * ye

    @pl.when(e == n_e - 1)
    def _():
        x2 = x1_ref[...] + g2_ref[...] * acc_sc[...]
        if final_norm:
            x2 = x2 * lax.rsqrt(jnp.mean(x2 * x2, axis=-1, keepdims=True) + EPS) * gf_ref[...]
        y_ref[...] = x2


def _moe(h2_bf, gate, w1, w3, w2, x1, g2, g_final, final_norm):
    b, s, d = x1.shape
    r = g2.shape[1]
    ne, _, f = w1.shape
    ts = _pick_tile(s, 1024)
    rb = 1 if r == 1 else ts
    tok = lambda bi, i, e: (bi, i, 0)
    mod = (lambda bi, i, e: (bi, 0, 0)) if r == 1 else tok
    return pl.pallas_call(
        functools.partial(_moe_kernel, final_norm=final_norm),
        out_shape=jax.ShapeDtypeStruct((b, s, d), F32),
        grid=(b, s // ts, ne),
        in_specs=[pl.BlockSpec((None, ts, d), tok),
                  pl.BlockSpec((None, ts, LANES), tok),
                  pl.BlockSpec((None, d, f), lambda bi, i, e: (e, 0, 0)),
                  pl.BlockSpec((None, d, f), lambda bi, i, e: (e, 0, 0)),
                  pl.BlockSpec((None, f, d), lambda bi, i, e: (e, 0, 0)),
                  pl.BlockSpec((None, ts, d), tok),
                  pl.BlockSpec((None, rb, d), mod),
                  pl.BlockSpec((1, d), lambda bi, i, e: (0, 0))],
        out_specs=pl.BlockSpec((None, ts, d), tok),
        scratch_shapes=[pltpu.VMEM((ts, d), F32)],
        compiler_params=_params(("arbitrary", "arbitrary", "arbitrary")),
        name="moe",
    )(h2_bf, gate, w1, w3, w2, x1, g2, g_final.reshape(1, d))


def _project(h_bf, w_in_bf):
    d = h_bf.shape[-1]
    m = h_bf.shape[0]
    da = HA * 2 * DK
    dml = HM * DH
    o = 0
    cols = {}
    for name, width in (("q", da), ("k", da), ("v", da), ("mqk", 2 * dml), ("mv", dml), ("mo", dml),
                        ("gates", 2 * HM), ("ga", d), ("gb", d)):
        cols[name] = (o, o + width)
        o += width
    w = lambda name: w_in_bf[:, cols[name][0]:cols[name][1]]
    (q_bf,) = _mm(h_bf, w("q"), [BF16], scale=DK ** -0.5)
    k_f, k_bf = _mm(h_bf, w("k"), [F32, BF16])
    v_f, v_bf = _mm(h_bf, w("v"), [F32, BF16])
    (mqk,) = _mm(h_bf, w("mqk"), [F32])
    (mv_bf,) = _mm(h_bf, w("mv"), [BF16])
    (mo,) = _mm(h_bf, w("mo"), [F32])
    w_g = jnp.zeros((d, LANES), BF16).at[:, :2 * HM].set(w("gates"))
    (gates,) = _mm(h_bf, w_g, [F32])
    (ga,) = _mm(h_bf, w("ga"), [F32])
    (gb,) = _mm(h_bf, w("gb"), [F32])
    del m
    return q_bf, k_f, k_bf, v_f, v_bf, mqk, mv_bf, mo, gates, ga, gb


def _layer_weights(l, w_in, w_pa, w_pb, w_o, w_router, w1, w3, w2, ws1, ws3, ws2):
    return dict(
        w_in=w_in[l].astype(BF16),
        w_pa=w_pa[l].astype(BF16), w_pb=w_pb[l].astype(BF16), w_o=w_o[l].astype(BF16),
        w_router_t=w_router[l].T,
        w1=jnp.concatenate([w1[l], ws1[l][None]], axis=0).astype(BF16),
        w3=jnp.concatenate([w3[l], ws3[l][None]], axis=0).astype(BF16),
        w2=jnp.concatenate([w2[l], ws2[l][None]], axis=0).astype(BF16),
    )


def kernel(x_prompt, x_sample, cache_k, cache_v, state_C, state_n, state_m, state_conv, page_table, c_prompt, c_sample, rel_table, w_ada, b_ada, g_norm1, w_in, b_if, lam_params, g_sub, w_conv, b_conv, g_mh, w_pa, w_pb, w_o, g_norm2, w_router, e_bias, w1, w3, w2, ws1, ws3, ws2, g_final):
    depth = w_in.shape[0]
    bp, sp, d = x_prompt.shape
    bs, ss, _ = x_sample.shape
    assert ss == 1, "the sample group advances one token per sequence"
    assert sp % ATTN_BLOCK == 0 and sp % MLSTM_CHUNK == 0
    n_pages = page_table.shape[1]
    past = n_pages * PAGE_SIZE
    da = HA * 2 * DK
    dml = HM * DH

    xp = x_prompt
    xs = x_sample.reshape(1, bs, d)
    outs = {k: [] for k in ("kp", "vp", "ks", "vs", "cp", "np", "mp", "convp", "cs", "ns", "ms", "convs")}

    for l in range(depth):
        lw = _layer_weights(l, w_in, w_pa, w_pb, w_o, w_router, w1, w3, w2, ws1, ws3, ws2)
        lam_init = 0.8 - 0.6 * math.exp(-0.3 * l)
        lp = lam_params[l].astype(F32)
        lam = (jnp.exp(jnp.sum(lp[0] * lp[1])) - jnp.exp(jnp.sum(lp[2] * lp[3])) + lam_init).reshape(1)

        ada = _ada(jnp.concatenate([c_prompt, c_sample], axis=0), w_ada[l], b_ada[l])
        ada_p = ada[:bp].reshape(bp, 1, 6, d)
        ada_s = ada[bp:].reshape(1, bs, 6, d)
        sh1p, sc1p, g1p, sh2p, sc2p, g2p = (ada_p[:, :, i] for i in range(6))
        sh1s, sc1s, g1s, sh2s, sc2s, g2s = (ada_s[:, :, i] for i in range(6))

        h_bf = _normmod(xp, g_norm1[l], sh1p, sc1p).reshape(bp * sp, d)
        q_bf, k_f, k_bf, v_f, v_bf, mqk, mv_bf, mo, gates, ga, gb = _project(h_bf, lw["w_in"])
        outs["kp"].append(k_f.reshape(bp, sp, HA, 2 * DK))
        outs["vp"].append(v_f.reshape(bp, sp, HA, DV))
        q4 = q_bf.reshape(bp, sp, HA, 2, DK)
        zero = jnp.zeros_like(q4[:, :, :, 0])
        qbd = jnp.stack([jnp.concatenate([q4[:, :, :, 0], zero], axis=-1),
                         jnp.concatenate([zero, q4[:, :, :, 1]], axis=-1)], axis=1)
        qbd = jnp.transpose(qbd, (0, 3, 1, 2, 4))
        vt = jnp.transpose(v_bf.reshape(bp, sp, HA, DV), (0, 2, 3, 1))
        bias_tiles = _prompt_bias_tiles(rel_table, ATTN_BLOCK)
        a_bf = _attn_prompt(qbd, k_bf.reshape(bp, sp, da), vt, bias_tiles, lam, g_sub[l], lam_init)

        nc = sp // MLSTM_CHUNK
        g3 = gates.reshape(bp, sp, LANES)
        g_rows = jnp.transpose(g3[:, :, :2 * HM].reshape(bp, nc, MLSTM_CHUNK, 2 * HM), (0, 1, 3, 2))
        mqk3 = mqk.reshape(bp, sp, 2 * dml)
        mo_bf, c_p, n_p, m_p = _mlstm_prompt(mqk3, mv_bf.reshape(bp, sp, dml), mo.reshape(bp, sp, dml),
                                             g_rows, g3, w_conv[l], b_conv[l], b_if[l], g_mh[l])
        outs["cp"].append(c_p)
        outs["np"].append(n_p)
        outs["mp"].append(m_p[:, :HM, 0])
        outs["convp"].append(mqk3[:, sp - (CONV_W - 1):, :])

        x1, h2_bf, gate = _mix(a_bf, mo_bf, ga.reshape(bp, sp, d), gb.reshape(bp, sp, d), xp, g1p, sh2p, sc2p,
                               lw["w_pa"], lw["w_pb"], lw["w_o"], g_norm2[l], lw["w_router_t"], e_bias[l])
        last = l == depth - 1
        xp = _moe(h2_bf, gate, lw["w1"], lw["w3"], lw["w2"], x1, g2p, g_final, last)

        hs_bf = _normmod(xs, g_norm1[l], sh1s, sc1s).reshape(bs, d)
        q_s, k_s, _, v_s, _, mqk_s, mv_s, mo_s, gates_s, ga_s, gb_s = _project(hs_bf, lw["w_in"])
        outs["ks"].append(k_s.reshape(bs, 1, HA, 2 * DK))
        outs["vs"].append(v_s.reshape(bs, 1, HA, DV))
        col_map = jnp.arange(da) // DK
        qmat = jnp.where(col_map[None, None, :] == jnp.arange(2 * HA)[None, :, None], q_s[:, None, :],
                         jnp.zeros((), BF16))
        tab = _shifted_table(rel_table)
        bt = _t5_bucket_table(past)
        rel_pages = past - np.arange(past).reshape(n_pages, PAGE_SIZE)
        bias_pages = jnp.repeat(jnp.transpose(tab[bt[rel_pages]], (0, 2, 1)), 2, axis=1)
        bias_new = jnp.repeat(tab[bt[0]], 2).reshape(2 * HA, 1)
        a_s = _attn_sample(qmat, k_s.reshape(bs, 1, da), v_s.reshape(bs, 1, da),
                           cache_k.reshape(-1, PAGE_SIZE, da), cache_v.reshape(-1, PAGE_SIZE, da),
                           page_table + l * cache_k.shape[1], bias_pages, bias_new, lam, g_sub[l], lam_init)

        mo_s_bf, c_s, n_s, m_s, conv_s = _mlstm_sample(
            mqk_s.reshape(bs, 1, 2 * dml), state_conv[l], mv_s.reshape(bs, 1, dml), mo_s.reshape(bs, 1, dml),
            gates_s.reshape(bs, 1, LANES), b_if[l], state_C[l], state_n[l], state_m[l].reshape(bs, 1, HM),
            w_conv[l], b_conv[l], g_mh[l])
        outs["cs"].append(c_s)
        outs["ns"].append(n_s)
        outs["ms"].append(m_s.reshape(bs, HM))
        outs["convs"].append(conv_s)

        x1s, h2s_bf, gate_s = _mix(a_s.reshape(1, bs, d), mo_s_bf.reshape(1, bs, d), ga_s.reshape(1, bs, d),
                                   gb_s.reshape(1, bs, d), xs, g1s, sh2s, sc2s,
                                   lw["w_pa"], lw["w_pb"], lw["w_o"], g_norm2[l], lw["w_router_t"], e_bias[l])
        xs = _moe(h2s_bf, gate_s, lw["w1"], lw["w3"], lw["w2"], x1s, g2s, g_final, last)

    st = lambda key: jnp.stack(outs[key])
    return (xp, xs.reshape(bs, 1, d), st("kp"), st("vp"), st("ks"), st("vs"),
            st("cp"), st("np"), st("mp"), st("convp"), st("cs"), st("ns"), st("ms"), st("convs"))
```
